```python
import math
import jax, jax.numpy as jnp
from jax import lax
import numpy as np

D_MODEL = 1024
BATCH = 2
SEQ = 8192
DEPTH = 1
DEC_BATCH = 32
DEC_SEQ = 8
PAST_LEN = 16384
PAGE_SIZE = 128

ATT_HEADS = 8
HEAD_DIM = 64
ATT_WIDTH = ATT_HEADS * HEAD_DIM
MOBA_BLOCK = 256
MOBA_TOPK = 3
ROPE_THETA = 10000.0
Q_BLOCK = 32
GM_GROUPS = 8
GM_DIM = 64
GM_WIDTH = GM_GROUPS * GM_DIM
GM_CHUNK = 128
PEER_HEADS = 8
PEER_QDIM = 256
N_KEYS = 128
N_EXPERTS = N_KEYS * N_KEYS
PEER_TOPK = 16
TOK_BLOCK = 256
PLE_DIM = 256
IN_COLS = 3 * ATT_WIDTH + 2 * GM_WIDTH + 2 * D_MODEL
EPS = 1e-6
NEG = -1e30

kernel_name = 'hybrid_moba_gmlp_peer_decode_step'


def _rmsnorm(x, g):
    xf = x.astype(jnp.float32)
    y = xf * lax.rsqrt(jnp.mean(xf * xf, axis=-1, keepdims=True) + EPS)
    return (y * g.astype(jnp.float32)).astype(x.dtype)


def _rope(x, pos):
    half = HEAD_DIM // 2
    inv = jnp.exp(-math.log(ROPE_THETA) * jnp.arange(half, dtype=jnp.float32) / half)
    ang = pos.astype(jnp.float32)[:, None] * inv[None, :]
    cos = jnp.cos(ang)[None, :, None, :]
    sin = jnp.sin(ang)[None, :, None, :]
    xf = x.astype(jnp.float32)
    x1, x2 = xf[..., :half], xf[..., half:]
    return jnp.concatenate([x1 * cos - x2 * sin, x2 * cos + x1 * sin], axis=-1).astype(x.dtype)


def _moba_attend(q, k, v, q_pos):
    b, l, h, dh = k.shape
    nq = q.shape[1]
    nb = -(-l // MOBA_BLOCK)
    pad = nb * MOBA_BLOCK - l
    kbh = jnp.pad(k, ((0, 0), (0, pad), (0, 0), (0, 0))).reshape(b, nb, MOBA_BLOCK, h, dh).transpose(0, 3, 1, 2, 4)
    vbh = jnp.pad(v, ((0, 0), (0, pad), (0, 0), (0, 0))).reshape(b, nb, MOBA_BLOCK, h, dh).transpose(0, 3, 1, 2, 4)
    kbar = jnp.mean(kbh.astype(jnp.float32), axis=3)
    nsc = max(nb, MOBA_TOPK)
    qblk = math.gcd(nq, Q_BLOCK)
    qs = q.reshape(b, nq // qblk, qblk, h, dh).transpose(1, 0, 2, 3, 4)
    ps = q_pos.reshape(nq // qblk, qblk)
    scale = HEAD_DIM ** -0.5
    bi = jnp.arange(b)[:, None, None, None]
    hi = jnp.arange(h)[None, :, None, None]

    def one(args):
        qc, pc = args
        own = pc // MOBA_BLOCK
        s = jnp.einsum('bqhd,bhnd->bhqn', qc.astype(jnp.float32), kbar)
        s = jnp.where(jnp.arange(nb)[None, :] < own[:, None], s, NEG)
        if nsc > nb:
            s = jnp.pad(s, ((0, 0), (0, 0), (0, 0), (0, nsc - nb)), constant_values=NEG)
        _, sel = lax.top_k(s, MOBA_TOPK)
        sel = jnp.minimum(sel, nb - 1)
        blocks = jnp.concatenate([sel, jnp.broadcast_to(own[None, None, :, None], (b, h, qblk, 1))], axis=-1)
        kg = kbh[bi, hi, blocks]
        vg = vbh[bi, hi, blocks]
        logits = jnp.einsum('bqhd,bhqnkd->bhqnk', qc, kg).astype(jnp.float32) * scale
        kpos = own[:, None] * MOBA_BLOCK + jnp.arange(MOBA_BLOCK)[None, :]
        sel_ok = jnp.arange(MOBA_TOPK)[None, :] < own[:, None]
        mask = jnp.concatenate([
            jnp.broadcast_to(sel_ok[:, :, None], (qblk, MOBA_TOPK, MOBA_BLOCK)),
            (kpos <= pc[:, None])[:, None, :]], axis=1)
        logits = jnp.where(mask[None, None], logits, NEG)
        pr = jax.nn.softmax(logits.reshape(b, h, qblk, -1), axis=-1).reshape(logits.shape).astype(v.dtype)
        return jnp.einsum('bhqnk,bhqnkd->bqhd', pr, vg)

    out = lax.map(one, (qs, ps))
    return out.transpose(1, 0, 2, 3, 4).reshape(b, nq, h, dh)


def _spatial_gate(u, v, ws, bias):
    b, t, _ = v.shape
    nc = -(-t // GM_CHUNK)
    pad = nc * GM_CHUNK - t
    vc = jnp.pad(v, ((0, 0), (0, pad), (0, 0))).reshape(b, nc, GM_CHUNK, GM_GROUPS, GM_DIM)
    w = ws * jnp.tril(jnp.ones((GM_CHUNK, GM_CHUNK), ws.dtype))[None]
    mixed = jnp.einsum('gts,bcsgd->bctgd', w, vc) + bias.T[None, None, :, :, None]
    mixed = mixed.reshape(b, nc * GM_CHUNK, GM_WIDTH)[:, :t]
    return u * mixed


def _peer(hn, wq, keys, eu, ev):
    n, d = hn.shape
    nblk = -(-n // TOK_BLOCK)
    hp = jnp.pad(hn, ((0, nblk * TOK_BLOCK - n), (0, 0))).reshape(nblk, TOK_BLOCK, d)
    half = PEER_QDIM // 2

    def one(hc):
        q = (hc @ wq).reshape(TOK_BLOCK, PEER_HEADS, 2, half).astype(jnp.float32)
        s = jnp.einsum('thsk,hsnk->thsn', q, keys.astype(jnp.float32))
        sv, si = lax.top_k(s, PEER_TOPK)
        cand = sv[:, :, 0, :, None] + sv[:, :, 1, None, :]
        cidx = si[:, :, 0, :, None] * N_KEYS + si[:, :, 1, None, :]
        top_s, top_j = lax.top_k(cand.reshape(TOK_BLOCK, PEER_HEADS, -1), PEER_TOPK)
        eidx = jnp.take_along_axis(cidx.reshape(TOK_BLOCK, PEER_HEADS, -1), top_j, axis=-1)
        g = jax.nn.softmax(top_s, axis=-1)
        ug = eu[eidx]
        vg = ev[eidx]
        a = jax.nn.gelu(jnp.einsum('td,thkd->thk', hc, ug).astype(jnp.float32))
        return jnp.einsum('thk,thkd->td', (g * a).astype(hc.dtype), vg)

    return lax.map(one, hp).reshape(nblk * TOK_BLOCK, d)[:n]


def _layer(x, p, pos0, past_k, past_v, norm1_g, w_in, q_norm_g, k_norm_g, gm_norm_g, gm_ws, gm_bias,
           w_br_a, w_br_b, w_out, norm2_g, peer_wq, peer_keys, peer_u, peer_v, ple_norm_g, ple_gate_w, ple_w):
    b, t, _ = x.shape
    pos = pos0 + jnp.arange(t, dtype=jnp.int32)
    h = _rmsnorm(x, norm1_g)
    z = h @ w_in
    a3 = 3 * ATT_WIDTH
    q, k, v, u, gv, ga, gb = jnp.split(
        z, [ATT_WIDTH, 2 * ATT_WIDTH, a3, a3 + GM_WIDTH, a3 + 2 * GM_WIDTH, a3 + 2 * GM_WIDTH + D_MODEL], axis=-1)
    q = _rope(_rmsnorm(q.reshape(b, t, ATT_HEADS, HEAD_DIM), q_norm_g), pos)
    k = _rope(_rmsnorm(k.reshape(b, t, ATT_HEADS, HEAD_DIM), k_norm_g), pos)
    v = v.reshape(b, t, ATT_HEADS, HEAD_DIM)
    if past_k is None:
        k_all, v_all = k, v
    else:
        k_all = jnp.concatenate([past_k, k], axis=1)
        v_all = jnp.concatenate([past_v, v], axis=1)
    att = _moba_attend(q, k_all, v_all, pos).reshape(b, t, ATT_WIDTH)
    gvn = _rmsnorm(jax.nn.gelu(gv), gm_norm_g)
    sg = _spatial_gate(jax.nn.gelu(u), gvn, gm_ws, gm_bias)
    merged = jax.nn.sigmoid(ga) * (att @ w_br_a) + jax.nn.sigmoid(gb) * (sg @ w_br_b)
    x = x + merged @ w_out
    h2 = _rmsnorm(x, norm2_g)
    x = x + _peer(h2.reshape(b * t, D_MODEL), peer_wq, peer_keys, peer_u, peer_v).reshape(b, t, D_MODEL)
    gate = jax.nn.sigmoid(_rmsnorm(x, ple_norm_g) @ ple_gate_w)
    x = x + gate * (p @ ple_w)
    start = ((t - 1) // GM_CHUNK) * GM_CHUNK
    return x, k, v, gvn[:, start:]


def setup_inputs(seed: int = 0) -> dict:
    key = jax.random.key(seed)
    ks = jax.random.split(key, 32)
    f32 = jnp.float32
    n_pages = PAST_LEN // PAGE_SIZE
    n_used = DEC_BATCH * n_pages
    n_pool = n_used + n_used // 4
    nrm = lambda k, shape, s: jax.random.normal(k, shape, f32) * s
    page_table = jax.random.permutation(ks[4], n_pool)[:n_used].reshape(DEC_BATCH, n_pages).astype(jnp.int32)
    return {
        'x_prompt': nrm(ks[0], (BATCH, SEQ, D_MODEL), 1.0),
        'x_sample': nrm(ks[1], (DEC_BATCH, DEC_SEQ, D_MODEL), 1.0),
        'cache_k': nrm(ks[2], (DEPTH, n_pool, PAGE_SIZE, ATT_HEADS, HEAD_DIM), 1.0),
        'cache_v': nrm(ks[3], (DEPTH, n_pool, PAGE_SIZE, ATT_HEADS, HEAD_DIM), 1.0),
        'page_table': page_table,
        'p_prompt': nrm(ks[5], (DEPTH, BATCH, SEQ, PLE_DIM), 1.0),
        'p_sample': nrm(ks[6], (DEPTH, DEC_BATCH, DEC_SEQ, PLE_DIM), 1.0),
        'norm1_g': 1.0 + nrm(ks[7], (DEPTH, D_MODEL), 0.02),
        'w_in': nrm(ks[8], (DEPTH, D_MODEL, IN_COLS), D_MODEL ** -0.5),
        'q_norm_g': 1.0 + nrm(ks[9], (DEPTH, HEAD_DIM), 0.02),
        'k_norm_g': 1.0 + nrm(ks[10], (DEPTH, HEAD_DIM), 0.02),
        'gm_norm_g': 1.0 + nrm(ks[11], (DEPTH, GM_WIDTH), 0.02),
        'gm_ws': nrm(ks[12], (DEPTH, GM_GROUPS, GM_CHUNK, GM_CHUNK), GM_CHUNK ** -0.5),
        'gm_bias': 1.0 + nrm(ks[13], (DEPTH, GM_GROUPS, GM_CHUNK), 0.02),
        'w_br_a': nrm(ks[14], (DEPTH, ATT_WIDTH, D_MODEL), ATT_WIDTH ** -0.5),
        'w_br_b': nrm(ks[15], (DEPTH, GM_WIDTH, D_MODEL), GM_WIDTH ** -0.5),
        'w_out': nrm(ks[16], (DEPTH, D_MODEL, D_MODEL), D_MODEL ** -0.5),
        'norm2_g': 1.0 + nrm(ks[17], (DEPTH, D_MODEL), 0.02),
        'peer_wq': nrm(ks[18], (DEPTH, D_MODEL, PEER_HEADS * PEER_QDIM), D_MODEL ** -0.5),
        'peer_keys': nrm(ks[19], (DEPTH, PEER_HEADS, 2, N_KEYS, PEER_QDIM // 2), (PEER_QDIM // 2) ** -0.5),
        'peer_u': nrm(ks[20], (DEPTH, N_EXPERTS, D_MODEL), D_MODEL ** -0.5),
        'peer_v': nrm(ks[21], (DEPTH, N_EXPERTS, D_MODEL), PEER_HEADS ** -0.5),
        'ple_norm_g': 1.0 + nrm(ks[22], (DEPTH, D_MODEL), 0.02),
        'ple_gate_w': nrm(ks[23], (DEPTH, D_MODEL, D_MODEL), D_MODEL ** -0.5),
        'ple_w': nrm(ks[24], (DEPTH, PLE_DIM, D_MODEL), PLE_DIM ** -0.5),
    }


def reference(x_prompt, x_sample, cache_k, cache_v, page_table, p_prompt, p_sample, norm1_g, w_in,
              q_norm_g, k_norm_g, gm_norm_g, gm_ws, gm_bias, w_br_a, w_br_b, w_out, norm2_g, peer_wq,
              peer_keys, peer_u, peer_v, ple_norm_g, ple_gate_w, ple_w):
    db = x_sample.shape[0]
    past = page_table.shape[1] * PAGE_SIZE
    y_p, y_s = x_prompt, x_sample
    kp_l, vp_l, ks_l, vs_l, gp_l, gs_l = [], [], [], [], [], []
    for l in range(DEPTH):
        w = (norm1_g[l], w_in[l], q_norm_g[l], k_norm_g[l], gm_norm_g[l], gm_ws[l], gm_bias[l],
             w_br_a[l], w_br_b[l], w_out[l], norm2_g[l], peer_wq[l], peer_keys[l], peer_u[l], peer_v[l],
             ple_norm_g[l], ple_gate_w[l], ple_w[l])
        pk = cache_k[l, page_table].reshape(db, past, ATT_HEADS, HEAD_DIM)
        pv = cache_v[l, page_table].reshape(db, past, ATT_HEADS, HEAD_DIM)
        y_p, kp, vp, gp = _layer(y_p, p_prompt[l], 0, None, None, *w)
        y_s, kn, vn, gn = _layer(y_s, p_sample[l], past, pk, pv, *w)
        kp_l.append(kp); vp_l.append(vp); ks_l.append(kn); vs_l.append(vn); gp_l.append(gp); gs_l.append(gn)
    return (y_p, y_s, jnp.stack(kp_l), jnp.stack(vp_l), jnp.stack(ks_l), jnp.stack(vs_l),
            jnp.stack(gp_l), jnp.stack(gs_l))
```

```python
import functools
import math

import jax
import jax.numpy as jnp
from jax import lax
from jax.experimental import pallas as pl
from jax.experimental.pallas import tpu as pltpu

F32 = jnp.float32
BF16 = jnp.bfloat16
I32 = jnp.int32

ATT_HEADS = 8
HEAD_DIM = 64
MOBA_BLOCK = 256
MOBA_TOPK = 3
ROPE_THETA = 10000.0
GM_GROUPS = 8
GM_DIM = 64
GM_CHUNK = 128
PEER_HEADS = 8
N_KEYS = 128
PEER_TOPK = 16
PAGE_SIZE = 128
EPS = 1e-6
NEG = -1e30

LANES = 128
SUBLANES = 8
V7X_VMEM_BYTES = 64 * 1024 * 1024

TOKEN_TILE = 256
PAGES_PER_STEP = 8
PEER_TOKEN_TILE = 256
PEER_EXPERT_ROWS = 4

_NT = (((1,), (1,)), ((), ()))
_TN = (((0,), (0,)), ((), ()))


def _vmem_limit(estimate_bytes):
    return int(min(max(2 * estimate_bytes, 32 * 1024 * 1024), V7X_VMEM_BYTES - 8 * 1024 * 1024))


def _params(semantics, estimate_bytes):
    return pltpu.CompilerParams(dimension_semantics=semantics, vmem_limit_bytes=_vmem_limit(estimate_bytes))


def _mod(x, n):
    return x & (n - 1) if n & (n - 1) == 0 else lax.rem(x, n)


def _rms(x, g):
    return x * lax.rsqrt(jnp.mean(x * x, axis=-1, keepdims=True) + EPS) * g


def _inproj_kernel(n_prompt_tiles, seq, past, dec_seq,
                   x_ref, g1_ref, w_ref, qg_ref, kg_ref, gmg_ref, inv_ref, sgn_ref, smat_ref,
                   q32_ref, qb_ref, k32_ref, kb_ref, v32_ref, vb_ref, gu_ref, gvn_ref, sga_ref, sgb_ref,
                   kbar_ref):
    i = pl.program_id(0)
    tm = x_ref.shape[0]
    aw = q32_ref.shape[1]
    gw = gu_ref.shape[1]
    d = x_ref.shape[1]
    h = _rms(x_ref[...], g1_ref[...])
    z = jnp.dot(h.astype(BF16), w_ref[...], preferred_element_type=F32)
    q = z[:, 0:aw]
    k = z[:, aw:2 * aw]
    v = z[:, 2 * aw:3 * aw]
    u = z[:, 3 * aw:3 * aw + gw]
    gv = z[:, 3 * aw + gw:3 * aw + 2 * gw]
    ga = z[:, 3 * aw + 2 * gw:3 * aw + 2 * gw + d]
    gb = z[:, 3 * aw + 2 * gw + d:3 * aw + 2 * gw + 2 * d]

    r = lax.broadcasted_iota(I32, (tm, LANES), 0)
    is_prompt = jnp.full((tm, LANES), i, I32) < n_prompt_tiles
    pos = jnp.where(is_prompt, _mod(i * tm, seq) + r, past + _mod(r, dec_seq))
    ang = pos.astype(F32) * inv_ref[...]
    cos = jnp.cos(ang)
    sin = jnp.sin(ang) * sgn_ref[...]
    reps = aw // LANES
    cos_w = jnp.concatenate([cos] * reps, axis=1)
    sin_w = jnp.concatenate([sin] * reps, axis=1)
    lane = lax.broadcasted_iota(I32, (tm, aw), 1)
    first_half = (lane & (HEAD_DIM - 1)) < HEAD_DIM // 2
    smat = smat_ref[...]

    def head_norm_rope(t, g):
        sq = t * t
        hi = sq.astype(BF16)
        lo = (sq - hi.astype(F32)).astype(BF16)
        ms = jnp.dot(hi, smat, preferred_element_type=F32) + jnp.dot(lo, smat, preferred_element_type=F32)
        tn = t * lax.rsqrt(ms + EPS) * g
        partner = jnp.where(first_half, pltpu.roll(tn, aw - HEAD_DIM // 2, 1), pltpu.roll(tn, HEAD_DIM // 2, 1))
        return tn * cos_w + partner * sin_w

    qr = head_norm_rope(q, qg_ref[...])
    kr = head_norm_rope(k, kg_ref[...])
    q32_ref[...] = qr
    qb_ref[...] = qr.astype(BF16)
    k32_ref[...] = kr
    kb_ref[...] = kr.astype(BF16)
    v32_ref[...] = v
    vb_ref[...] = v.astype(BF16)
    gu_ref[...] = jax.nn.gelu(u)
    gvn_ref[...] = _rms(jax.nn.gelu(gv), gmg_ref[...])
    sga_ref[...] = jax.nn.sigmoid(ga)
    sgb_ref[...] = jax.nn.sigmoid(gb)
    kbar_ref[0] = jnp.mean(kr, axis=0, keepdims=True)


def _inproj(x, g1, w_in_b, qg, kg, gmg, n_prompt_tiles, seq, past, dec_seq):
    n, d = x.shape
    aw = ATT_HEADS * HEAD_DIM
    gw = GM_GROUPS * GM_DIM
    cols = w_in_b.shape[1]
    tm = TOKEN_TILE
    nt = n // tm
    half = HEAD_DIM // 2
    inv = jnp.exp(-math.log(ROPE_THETA) * jnp.arange(half, dtype=F32) / half)
    inv_l = jnp.tile(inv, LANES // half)[None, :]
    sgn_l = jnp.tile(jnp.concatenate([-jnp.ones((half,), F32), jnp.ones((half,), F32)]), LANES // HEAD_DIM)[None, :]
    head_of = jnp.arange(aw) // HEAD_DIM
    smat = jnp.where(head_of[:, None] == head_of[None, :], 1.0 / HEAD_DIM, 0.0).astype(BF16)
    tok = lambda w: pl.BlockSpec((tm, w), lambda i: (i, 0))
    full = lambda a: pl.BlockSpec(a.shape, lambda i: (0,) * a.ndim)
    qg_w = jnp.tile(qg, ATT_HEADS)[None, :]
    kg_w = jnp.tile(kg, ATT_HEADS)[None, :]
    ins = (x, g1[None, :], w_in_b, qg_w, kg_w, gmg[None, :], inv_l, sgn_l, smat)
    out_shape = (
        jax.ShapeDtypeStruct((n, aw), F32), jax.ShapeDtypeStruct((n, aw), BF16),
        jax.ShapeDtypeStruct((n, aw), F32), jax.ShapeDtypeStruct((n, aw), BF16),
        jax.ShapeDtypeStruct((n, aw), F32), jax.ShapeDtypeStruct((n, aw), BF16),
        jax.ShapeDtypeStruct((n, gw), F32), jax.ShapeDtypeStruct((n, gw), F32),
        jax.ShapeDtypeStruct((n, d), F32), jax.ShapeDtypeStruct((n, d), F32),
        jax.ShapeDtypeStruct((nt, 1, aw), F32),
    )
    out_specs = (tok(aw),) * 6 + (tok(gw),) * 2 + (tok(d),) * 2 + (pl.BlockSpec((1, 1, aw), lambda i: (i, 0, 0)),)
    est = 2 * w_in_b.size * 2 + 2 * tm * d * 4 + 3 * tm * cols * 4 + 2 * tm * cols * 4
    return pl.pallas_call(
        functools.partial(_inproj_kernel, n_prompt_tiles, seq, past, dec_seq),
        grid=(nt,),
        in_specs=[tok(d)] + [full(a) for a in ins[1:]],
        out_specs=out_specs,
        out_shape=out_shape,
        compiler_params=_params(("arbitrary",), est),
        name="inproj",
    )(*ins)


def _topk_mask(s, col, n_valid, k):
    sel = jnp.zeros(s.shape, F32)
    width = s.shape[1]
    for r in range(k):
        m = jnp.max(s, axis=1, keepdims=True)
        idx = jnp.min(jnp.where(s == m, col, width), axis=1, keepdims=True)
        hit = col == idx
        sel = jnp.where(jnp.logical_and(hit, n_valid > r), 1.0, sel)
        s = jnp.where(hit, -jnp.inf, s)
    return sel


def _moba_prompt_kernel(q32_ref, qb_ref, kb_ref, vb_ref, kbar_ref, o_ref):
    qi = pl.program_id(2)
    tq = q32_ref.shape[0]
    nb = kbar_ref.shape[1]
    scale = HEAD_DIM ** -0.5
    lane = lax.broadcasted_iota(I32, (1, LANES), 1)
    lo_lanes = lane < HEAD_DIM
    q32 = q32_ref[...]
    qb = qb_ref[...]
    kbar = kbar_ref[0]
    col = lax.broadcasted_iota(I32, (tq, nb), 1)
    n_past = jnp.full((tq, nb), qi, I32)
    row_q = lax.broadcasted_iota(I32, (tq, tq), 0)
    col_k = lax.broadcasted_iota(I32, (tq, tq), 1)
    own0 = pl.multiple_of(qi * tq, tq)
    k_own = kb_ref[pl.ds(own0, tq), :]
    v_own = vb_ref[pl.ds(own0, tq), :]
    outs = []
    for hh in range(2):
        head_lanes = lo_lanes if hh == 0 else jnp.logical_not(lo_lanes)
        s = lax.dot_general(jnp.where(head_lanes, q32, 0.0), kbar, _NT,
                            precision=lax.Precision.HIGHEST, preferred_element_type=F32)
        s = jnp.where(col < n_past, s, NEG)
        sel = _topk_mask(s, col, n_past, MOBA_TOPK)
        qh = jnp.where(head_lanes, qb, jnp.zeros_like(qb))

        lg = lax.dot_general(qh, k_own, _NT, preferred_element_type=F32) * scale
        lg = jnp.where(col_k <= row_q, lg, NEG)
        m0 = jnp.max(lg, axis=1, keepdims=True)
        p0 = jnp.exp(lg - m0)
        l0 = jnp.sum(p0, axis=1, keepdims=True)
        acc0 = jnp.dot(p0.astype(BF16), v_own, preferred_element_type=F32)

        def body(n, carry, qh=qh, sel=sel):
            m, l, acc = carry
            off = pl.multiple_of(n * tq, tq)
            kn = kb_ref[pl.ds(off, tq), :]
            vn = vb_ref[pl.ds(off, tq), :]
            lg = lax.dot_general(qh, kn, _NT, preferred_element_type=F32) * scale
            seln = jnp.max(jnp.where(col == n, sel, 0.0), axis=1, keepdims=True)
            lg = jnp.where(seln > 0.0, lg, NEG)
            m_new = jnp.maximum(m, jnp.max(lg, axis=1, keepdims=True))
            alpha = jnp.exp(m - m_new)
            p = jnp.exp(lg - m_new)
            l = alpha * l + jnp.sum(p, axis=1, keepdims=True)
            acc = alpha * acc + jnp.dot(p.astype(BF16), vn, preferred_element_type=F32)
            return m_new, l, acc

        _, l, acc = lax.fori_loop(0, qi, body, (m0, l0, acc0))
        outs.append(acc / l)
    o_ref[...] = jnp.where(lo_lanes, outs[0], outs[1]).astype(o_ref.dtype)


def _moba_prompt(q32, qb, kb, vb, kbar, batch, seq):
    aw = ATT_HEADS * HEAD_DIM
    tq = MOBA_BLOCK
    nb = seq // tq
    n_pairs = aw // LANES
    qspec = pl.BlockSpec((tq, LANES), lambda b, hp, qi: (b * nb + qi, hp))
    kvspec = pl.BlockSpec((seq, LANES), lambda b, hp, qi: (b, hp))
    est = 2 * 2 * seq * LANES * 2 + 8 * tq * tq * 4
    return pl.pallas_call(
        _moba_prompt_kernel,
        grid=(batch, n_pairs, nb),
        in_specs=[qspec, qspec, kvspec, kvspec, pl.BlockSpec((1, nb, LANES), lambda b, hp, qi: (b, 0, hp))],
        out_specs=qspec,
        out_shape=jax.ShapeDtypeStruct((batch * seq, aw), BF16),
        compiler_params=_params(("arbitrary",) * 3, est),
        name="moba_prompt",
    )(q32, qb, kb, vb, kbar)


def _page_index(j, b, s, pt_ref):
    return (pt_ref[b, s * PAGES_PER_STEP + j], 0, 0)


def _cache_kbar_kernel(pt_ref, *refs):
    pages, o_ref = refs[:PAGES_PER_STEP], refs[PAGES_PER_STEP]
    ppb = MOBA_BLOCK // PAGE_SIZE
    for blk in range(PAGES_PER_STEP // ppb):
        tot = None
        for j in range(blk * ppb, (blk + 1) * ppb):
            part = pages[j][...].reshape(PAGE_SIZE, ATT_HEADS, HEAD_DIM).sum(axis=0)
            tot = part if tot is None else tot + part
        o_ref[0, blk] = tot * (1.0 / MOBA_BLOCK)


def _cache_kbar(cache_pages, page_table):
    db, n_pages = page_table.shape
    ppb = MOBA_BLOCK // PAGE_SIZE
    page_rows = PAGE_SIZE * ATT_HEADS
    steps = n_pages // PAGES_PER_STEP
    grid_spec = pltpu.PrefetchScalarGridSpec(
        num_scalar_prefetch=1,
        grid=(db, steps),
        in_specs=[pl.BlockSpec((None, page_rows, HEAD_DIM), functools.partial(_page_index, j))
                  for j in range(PAGES_PER_STEP)],
        out_specs=pl.BlockSpec((1, PAGES_PER_STEP // ppb, ATT_HEADS, HEAD_DIM), lambda b, s, pt: (b, s, 0, 0)),
    )
    est = 2 * PAGES_PER_STEP * page_rows * LANES * 4
    return pl.pallas_call(
        _cache_kbar_kernel,
        grid_spec=grid_spec,
        out_shape=jax.ShapeDtypeStruct((db, n_pages // ppb, ATT_HEADS, HEAD_DIM), F32),
        compiler_params=_params(("arbitrary",) * 2, est),
        name="cache_kbar",
    )(page_table, *([cache_pages] * PAGES_PER_STEP))


def _moba_sample_kernel(pt_ref, q_ref, knew_ref, vnew_ref, kbar_ref, *refs):
    g = PAGES_PER_STEP
    kp, vp = refs[:g], refs[g:2 * g]
    o_ref = refs[2 * g]
    sel_ref, m_ref, l_ref, acc_ref = refs[2 * g + 1:]
    s = pl.program_id(1)
    tq = q_ref.shape[0]
    nbp = sel_ref.shape[2]
    scale = HEAD_DIM ** -0.5
    ppb = MOBA_BLOCK // PAGE_SIZE
    colb = lax.broadcasted_iota(I32, (tq, nbp), 1)
    heads = [slice(h * HEAD_DIM, (h + 1) * HEAD_DIM) for h in range(ATT_HEADS)]
    q = q_ref[...]
    q_b = q.astype(BF16)

    @pl.when(s == 0)
    def _init():
        row = lax.broadcasted_iota(I32, (tq, tq), 0)
        colq = lax.broadcasted_iota(I32, (tq, tq), 1)
        n_past = jnp.full((tq, nbp), nbp, I32)
        knew = knew_ref[...].astype(BF16)
        vnew = vnew_ref[...].astype(BF16)
        for h in range(ATT_HEADS):
            kbar_h = kbar_ref[pl.ds(h, nbp, stride=ATT_HEADS), :]
            sc = lax.dot_general(q[:, heads[h]], kbar_h, _NT, precision=lax.Precision.HIGHEST,
                                 preferred_element_type=F32)
            sel_ref[h] = _topk_mask(sc, colb, n_past, MOBA_TOPK)
            lg = lax.dot_general(q_b[:, heads[h]], knew[:, heads[h]], _NT, preferred_element_type=F32) * scale
            lg = jnp.where(colq <= row, lg, NEG)
            m0 = jnp.max(lg, axis=1, keepdims=True)
            p0 = jnp.exp(lg - m0)
            m_ref[h] = m0
            l_ref[h] = jnp.sum(p0, axis=1, keepdims=True)
            acc_ref[h] = jnp.dot(p0.astype(BF16), vnew[:, heads[h]], preferred_element_type=F32)

    for j in range(g):
        n = (s * g + j) // ppb
        for h in range(ATT_HEADS):
            kh = kp[j][pl.ds(h, PAGE_SIZE, stride=ATT_HEADS), :].astype(BF16)
            vh = vp[j][pl.ds(h, PAGE_SIZE, stride=ATT_HEADS), :].astype(BF16)
            qh = q_b[:, heads[h]]
            lg = lax.dot_general(qh, kh, _NT, preferred_element_type=F32) * scale
            seln = jnp.max(jnp.where(colb == n, sel_ref[h], 0.0), axis=1, keepdims=True)
            lg = jnp.where(seln > 0.0, lg, NEG)
            m_old = m_ref[h]
            m_new = jnp.maximum(m_old, jnp.max(lg, axis=1, keepdims=True))
            alpha = jnp.exp(m_old - m_new)
            p = jnp.exp(lg - m_new)
            l_ref[h] = alpha * l_ref[h] + jnp.sum(p, axis=1, keepdims=True)
            acc_ref[h] = alpha * acc_ref[h] + jnp.dot(p.astype(BF16), vh, preferred_element_type=F32)
            m_ref[h] = m_new

    @pl.when(s == pl.num_programs(1) - 1)
    def _fin():
        o_ref[...] = jnp.concatenate([acc_ref[h] / l_ref[h] for h in range(ATT_HEADS)], axis=1).astype(o_ref.dtype)


def _moba_sample(q32, k32, v32, kbar_s, cache_k_pages, cache_v_pages, page_table, row0, dec_seq):
    db, n_pages = page_table.shape
    aw = ATT_HEADS * HEAD_DIM
    nbp = kbar_s.shape[1]
    page_rows = PAGE_SIZE * ATT_HEADS
    steps = n_pages // PAGES_PER_STEP
    blk0 = row0 // dec_seq
    tok = pl.BlockSpec((dec_seq, aw), lambda b, s, pt: (blk0 + b, 0))
    page = lambda j: pl.BlockSpec((None, page_rows, HEAD_DIM), functools.partial(_page_index, j))
    grid_spec = pltpu.PrefetchScalarGridSpec(
        num_scalar_prefetch=1,
        grid=(db, steps),
        in_specs=[tok, tok, tok, pl.BlockSpec((None, nbp * ATT_HEADS, HEAD_DIM), lambda b, s, pt: (b, 0, 0))]
        + [page(j) for j in range(PAGES_PER_STEP)] * 2,
        out_specs=pl.BlockSpec((dec_seq, aw), lambda b, s, pt: (b, 0)),
        scratch_shapes=[pltpu.VMEM((ATT_HEADS, dec_seq, nbp), F32), pltpu.VMEM((ATT_HEADS, dec_seq, 1), F32),
                        pltpu.VMEM((ATT_HEADS, dec_seq, 1), F32), pltpu.VMEM((ATT_HEADS, dec_seq, HEAD_DIM), F32)],
    )
    est = 2 * 2 * PAGES_PER_STEP * page_rows * LANES * 4
    return pl.pallas_call(
        _moba_sample_kernel,
        grid_spec=grid_spec,
        out_shape=jax.ShapeDtypeStruct((db * dec_seq, aw), BF16),
        compiler_params=_params(("arbitrary",) * 2, est),
        name="moba_sample",
    )(page_table, q32, k32, v32, kbar_s.reshape(db, nbp * ATT_HEADS, HEAD_DIM),
      *([cache_k_pages] * PAGES_PER_STEP), *([cache_v_pages] * PAGES_PER_STEP))


def _merge_kernel(n_prompt_chunks, dec_seq,
                  x_ref, att_ref, gu_ref, gvn_ref, sga_ref, sgb_ref, ws_ref, bias_ref,
                  wa_ref, wb_ref, wo_ref, g2_ref, wq_ref, keys_ref,
                  x1_ref, h2_ref, st_ref):
    i = pl.program_id(0)
    c = x_ref.shape[0]
    gw = gu_ref.shape[1]
    r = lax.broadcasted_iota(I32, (c, c), 0)
    cc = lax.broadcasted_iota(I32, (c, c), 1)
    is_prompt = jnp.full((c, c), i, I32) < n_prompt_chunks
    same_seq = (r // dec_seq) == (cc // dec_seq)
    keep = jnp.logical_and(cc <= r, jnp.logical_or(is_prompt, same_seq))
    lane = lax.broadcasted_iota(I32, (1, LANES), 1)
    lo_lanes = lane < GM_DIM
    gvn_b = gvn_ref[...].astype(BF16)
    mixed = []
    for pr in range(gw // LANES):
        v_pair = gvn_b[:, pr * LANES:(pr + 1) * LANES]
        halves = []
        for hh in range(2):
            w = jnp.where(keep, ws_ref[2 * pr + hh], 0.0).astype(BF16)
            halves.append(jnp.dot(w, v_pair, preferred_element_type=F32))
        mixed.append(jnp.where(lo_lanes, halves[0], halves[1]))
    mixed = jnp.concatenate(mixed, axis=1) + bias_ref[...]
    sg = gu_ref[...] * mixed
    a = jnp.dot(att_ref[...], wa_ref[...], preferred_element_type=F32)
    b = jnp.dot(sg.astype(BF16), wb_ref[...], preferred_element_type=F32)
    merged = sga_ref[...] * a + sgb_ref[...] * b
    x1 = x_ref[...] + jnp.dot(merged.astype(BF16), wo_ref[...], preferred_element_type=F32)
    x1_ref[...] = x1
    h2 = _rms(x1, g2_ref[...]).astype(BF16)
    h2_ref[...] = h2
    qp = jnp.dot(h2, wq_ref[...], preferred_element_type=F32)
    kd = keys_ref.shape[2]
    for hs in range(st_ref.shape[0]):
        st_ref[hs] = lax.dot_general(keys_ref[hs], qp[:, hs * kd:(hs + 1) * kd], _NT,
                                     precision=lax.Precision.HIGHEST, preferred_element_type=F32)


def _merge(x, att, gu, gvn, sga, sgb, ws2, bias2, wa, wb, wo, g2, wq, keys, n_prompt_chunks, dec_seq):
    n, d = x.shape
    c = GM_CHUNK
    aw = att.shape[1]
    gw = gu.shape[1]
    nhs = keys.shape[0]
    tok = lambda w: pl.BlockSpec((c, w), lambda i: (i, 0))
    full = lambda a: pl.BlockSpec(a.shape, lambda i: (0,) * a.ndim)
    variant = lambda i: jnp.where(i >= n_prompt_chunks, 1, 0)
    in_specs = [tok(d), tok(aw), tok(gw), tok(gw), tok(d), tok(d),
                pl.BlockSpec((None, GM_GROUPS, c, c), lambda i: (variant(i), 0, 0, 0)),
                pl.BlockSpec((None, c, gw), lambda i: (variant(i), 0, 0)),
                full(wa), full(wb), full(wo), full(g2), full(wq), full(keys)]
    out_shape = (jax.ShapeDtypeStruct((n, d), F32), jax.ShapeDtypeStruct((n, d), BF16),
                 jax.ShapeDtypeStruct((nhs, N_KEYS, n), F32))
    out_specs = (tok(d), tok(d), pl.BlockSpec((nhs, N_KEYS, c), lambda i: (0, 0, i)))
    est = 2 * 2 * (wa.size + wb.size + wo.size + wq.size) + 4 * keys.size * 2 + 16 * c * d * 4
    return pl.pallas_call(
        functools.partial(_merge_kernel, n_prompt_chunks, dec_seq),
        grid=(n // c,),
        in_specs=in_specs,
        out_specs=out_specs,
        out_shape=out_shape,
        compiler_params=_params(("arbitrary",), est),
        name="merge",
    )(x, att, gu, gvn, sga, sgb, ws2, bias2, wa, wb, wo, g2, wq, keys)


def _extract_top(vals, row, k):
    out = []
    height = vals.shape[0]
    for _ in range(k):
        m = jnp.max(vals, axis=0, keepdims=True)
        idx = jnp.min(jnp.where(vals == m, row, height), axis=0, keepdims=True)
        vals = jnp.where(row == idx, -jnp.inf, vals)
        out.append(m)
    return out


_PAIR_RANKS = [(a, b) for a in range(PEER_TOPK) for b in range(PEER_TOPK) if (a + 1) * (b + 1) <= PEER_TOPK]


_N_CAND_ROWS = -(-len(_PAIR_RANKS) // SUBLANES) * SUBLANES


def _peer_select_kernel(st_ref, e_ref, tau_ref, cand_ref):
    tn = st_ref.shape[2]
    row = lax.broadcasted_iota(I32, (N_KEYS, tn), 0)
    crow = lax.broadcasted_iota(I32, (_N_CAND_ROWS, tn), 0)
    cand_ref[...] = jnp.full((_N_CAND_ROWS, tn), -jnp.inf, F32)
    for h in range(PEER_HEADS):
        s0 = st_ref[2 * h]
        s1 = st_ref[2 * h + 1]
        top0 = _extract_top(s0, row, PEER_TOPK)
        top1 = _extract_top(s1, row, PEER_TOPK)
        for c, (a, b) in enumerate(_PAIR_RANKS):
            cand_ref[c:c + 1, :] = top0[a] + top1[b]
        best = _extract_top(cand_ref[...], crow, PEER_TOPK)
        z = jnp.zeros((1, tn), F32)
        for t in best:
            z = z + jnp.exp(t - best[0])
        e_ref[2 * h] = jnp.exp(s0 - top0[0]) / z
        e_ref[2 * h + 1] = jnp.exp(s1 - top1[0])
        tau_ref[h:h + 1, :] = best[-1]


def _peer_select(st):
    nhs, nk, n = st.shape
    tn = LANES
    spec = pl.BlockSpec((nhs, nk, tn), lambda i: (0, 0, i))
    return pl.pallas_call(
        _peer_select_kernel,
        grid=(n // tn,),
        in_specs=[spec],
        out_specs=(spec, pl.BlockSpec((PEER_HEADS, tn), lambda i: (0, i))),
        out_shape=(jax.ShapeDtypeStruct(st.shape, F32), jax.ShapeDtypeStruct((PEER_HEADS, n), F32)),
        scratch_shapes=[pltpu.VMEM((_N_CAND_ROWS, tn), F32)],
        compiler_params=_params(("arbitrary",), 8 * nhs * nk * tn * 4),
        name="peer_select",
    )(st)


def _peer_dense_kernel(h2_ref, st_ref, e_ref, tau_ref, u_ref, v_ref, o_ref, acc_ref):
    e = pl.program_id(1)
    tn = h2_ref.shape[0]

    @pl.when(e == 0)
    def _zero():
        acc_ref[...] = jnp.zeros_like(acc_ref)

    act = jax.nn.gelu(lax.dot_general(u_ref[...], h2_ref[...], _NT, preferred_element_type=F32))
    parts = []
    for il in range(PEER_EXPERT_ROWS):
        i = e * PEER_EXPERT_ROWS + il
        w = jnp.zeros((N_KEYS, tn), F32)
        for h in range(PEER_HEADS):
            s0_row = st_ref[2 * h, pl.ds(i, 1), :]
            e0_row = e_ref[2 * h, pl.ds(i, 1), :]
            taken = (s0_row + st_ref[2 * h + 1]) >= tau_ref[h:h + 1, :]
            w = w + jnp.where(taken, e0_row * e_ref[2 * h + 1], 0.0)
        parts.append((w * act[il * N_KEYS:(il + 1) * N_KEYS, :]).astype(BF16))
    p = jnp.concatenate(parts, axis=0)
    acc_ref[...] += lax.dot_general(p, v_ref[...], _TN, preferred_element_type=F32)

    @pl.when(e == pl.num_programs(1) - 1)
    def _out():
        o_ref[...] = acc_ref[...]


def _peer_dense(h2, st, ew, tau, eu_b, ev_b):
    n, d = h2.shape
    nhs, nk, _ = st.shape
    n_exp = eu_b.shape[0]
    tn = PEER_TOKEN_TILE
    te = PEER_EXPERT_ROWS * N_KEYS
    sspec = pl.BlockSpec((nhs, nk, tn), lambda t, e: (0, 0, t))
    wspec = pl.BlockSpec((te, d), lambda t, e: (e, 0))
    est = 2 * 2 * nhs * nk * tn * 4 + 2 * 2 * te * d * 2 + 3 * tn * d * 4 + 4 * te * tn * 4
    return pl.pallas_call(
        _peer_dense_kernel,
        grid=(n // tn, n_exp // te),
        in_specs=[pl.BlockSpec((tn, d), lambda t, e: (t, 0)), sspec, sspec,
                  pl.BlockSpec((PEER_HEADS, tn), lambda t, e: (0, t)), wspec, wspec],
        out_specs=pl.BlockSpec((tn, d), lambda t, e: (t, 0)),
        out_shape=jax.ShapeDtypeStruct((n, d), F32),
        scratch_shapes=[pltpu.VMEM((tn, d), F32)],
        compiler_params=_params(("arbitrary",) * 2, est),
        name="peer_dense",
    )(h2, st, ew, tau, eu_b, ev_b)


def _ple_kernel(x1_ref, peer_ref, p_ref, g_ref, wg_ref, wp_ref, o_ref):
    x2 = x1_ref[...] + peer_ref[...]
    gate = jax.nn.sigmoid(jnp.dot(_rms(x2, g_ref[...]).astype(BF16), wg_ref[...], preferred_element_type=F32))
    emb = jnp.dot(p_ref[...].astype(BF16), wp_ref[...], preferred_element_type=F32)
    o_ref[...] = x2 + gate * emb


def _ple(x1, peer, p, g, wg, wp):
    n, d = x1.shape
    pd = p.shape[1]
    tm = TOKEN_TILE
    tok = lambda w: pl.BlockSpec((tm, w), lambda i: (i, 0))
    full = lambda a: pl.BlockSpec(a.shape, lambda i: (0,) * a.ndim)
    est = 2 * 2 * (wg.size + wp.size) + 8 * tm * d * 4
    return pl.pallas_call(
        _ple_kernel,
        grid=(n // tm,),
        in_specs=[tok(d), tok(d), tok(pd), full(g), full(wg), full(wp)],
        out_specs=tok(d),
        out_shape=jax.ShapeDtypeStruct((n, d), F32),
        compiler_params=_params(("arbitrary",), est),
        name="ple",
    )(x1, peer, p, g, wg, wp)


def _layer(x, p, cache_k_l, cache_v_l, page_table, dims, w):
    (norm1_g, w_in, q_norm_g, k_norm_g, gm_norm_g, gm_ws, gm_bias, w_br_a, w_br_b, w_out, norm2_g,
     peer_wq, peer_keys, peer_u, peer_v, ple_norm_g, ple_gate_w, ple_w) = w
    batch, seq, db, dec_seq, past = dims
    n_prompt = batch * seq
    gw = GM_GROUPS * GM_DIM

    (q32, qb, k32, kb, v32, vb, gu, gvn, sga, sgb, kbar) = _inproj(
        x, norm1_g, w_in.astype(BF16), q_norm_g, k_norm_g, gm_norm_g,
        n_prompt // TOKEN_TILE, seq, past, dec_seq)

    nb = seq // MOBA_BLOCK
    att_p = _moba_prompt(q32, qb, kb, vb, kbar[:batch * nb].reshape(batch, nb, -1), batch, seq)
    n_pool = cache_k_l.shape[0]
    ck = cache_k_l.reshape(n_pool, PAGE_SIZE * ATT_HEADS, HEAD_DIM)
    cv = cache_v_l.reshape(n_pool, PAGE_SIZE * ATT_HEADS, HEAD_DIM)
    kbar_s = _cache_kbar(ck, page_table)
    att_s = _moba_sample(q32, k32, v32, kbar_s, ck, cv, page_table, n_prompt, dec_seq)
    att = jnp.concatenate([att_p, att_s], axis=0)

    reps = GM_CHUNK // dec_seq
    ws2 = jnp.stack([gm_ws, jnp.tile(gm_ws[:, :dec_seq, :dec_seq], (1, reps, reps))])
    bias_rows = jnp.stack([gm_bias, jnp.tile(gm_bias[:, :dec_seq], (1, reps))])
    bias2 = jnp.broadcast_to(bias_rows.transpose(0, 2, 1)[:, :, :, None], (2, GM_CHUNK, GM_GROUPS, GM_DIM))
    bias2 = bias2.reshape(2, GM_CHUNK, gw)
    keys = peer_keys.reshape(PEER_HEADS * 2, N_KEYS, -1)

    x1, h2, st = _merge(x, att, gu, gvn, sga, sgb, ws2, bias2, w_br_a.astype(BF16), w_br_b.astype(BF16),
                        w_out.astype(BF16), norm2_g[None, :], peer_wq.astype(BF16), keys,
                        n_prompt // GM_CHUNK, dec_seq)
    ew, tau = _peer_select(st)
    peer = _peer_dense(h2, st, ew, tau, peer_u.astype(BF16), peer_v.astype(BF16))
    y = _ple(x1, peer, p, ple_norm_g[None, :], ple_gate_w.astype(BF16), ple_w.astype(BF16))
    return y, k32, v32, gvn


def kernel(x_prompt, x_sample, cache_k, cache_v, page_table, p_prompt, p_sample, norm1_g, w_in, q_norm_g, k_norm_g, gm_norm_g, gm_ws, gm_bias, w_br_a, w_br_b, w_out, norm2_g, peer_wq, peer_keys, peer_u, peer_v, ple_norm_g, ple_gate_w, ple_w):
    batch, seq, d = x_prompt.shape
    db, dec_seq, _ = x_sample.shape
    depth = w_in.shape[0]
    n_pages = page_table.shape[1]
    past = n_pages * PAGE_SIZE
    n_prompt, n_sample = batch * seq, db * dec_seq
    gw = GM_GROUPS * GM_DIM
    assert seq % MOBA_BLOCK == 0 and TOKEN_TILE == MOBA_BLOCK and n_sample % TOKEN_TILE == 0
    assert past % MOBA_BLOCK == 0 and dec_seq <= MOBA_BLOCK and past // MOBA_BLOCK >= MOBA_TOPK
    assert GM_CHUNK % dec_seq == 0 and dec_seq % SUBLANES == 0 and n_pages % PAGES_PER_STEP == 0
    assert peer_keys.shape[3] == N_KEYS and peer_u.shape[1] == N_KEYS * N_KEYS
    dims = (batch, seq, db, dec_seq, past)

    x = jnp.concatenate([x_prompt.reshape(n_prompt, d), x_sample.reshape(n_sample, d)], axis=0)
    outs = ([], [], [], [], [], [])
    for l in range(depth):
        w = (norm1_g[l], w_in[l], q_norm_g[l], k_norm_g[l], gm_norm_g[l], gm_ws[l], gm_bias[l], w_br_a[l],
             w_br_b[l], w_out[l], norm2_g[l], peer_wq[l], peer_keys[l], peer_u[l], peer_v[l], ple_norm_g[l],
             ple_gate_w[l], ple_w[l])
        p = jnp.concatenate([p_prompt[l].reshape(n_prompt, -1), p_sample[l].reshape(n_sample, -1)], axis=0)
        x, k32, v32, gvn = _layer(x, p, cache_k[l], cache_v[l], page_table, dims, w)
        start = ((seq - 1) // GM_CHUNK) * GM_CHUNK
        outs[0].append(k32[:n_prompt].reshape(batch, seq, ATT_HEADS, HEAD_DIM))
        outs[1].append(v32[:n_prompt].reshape(batch, seq, ATT_HEADS, HEAD_DIM))
        outs[2].append(k32[n_prompt:].reshape(db, dec_seq, ATT_HEADS, HEAD_DIM))
        outs[3].append(v32[n_prompt:].reshape(db, dec_seq, ATT_HEADS, HEAD_DIM))
        outs[4].append(gvn[:n_prompt].reshape(batch, seq, gw)[:, start:])
        outs[5].append(gvn[n_prompt:].reshape(db, dec_seq, gw))
    y_p = x[:n_prompt].reshape(batch, seq, d)
    y_s = x[n_prompt:].reshape(db, dec_seq, d)
    return (y_p, y_s) + tuple(jnp.stack(o) for o in outs)
```

```python
import functools
import math

import jax
import jax.numpy as jnp
from jax import lax
from jax.experimental import pallas as pl
from jax.experimental.pallas import tpu as pltpu

F32 = jnp.float32
BF16 = jnp.bfloat16
I32 = jnp.int32

ATT_HEADS = 8
HEAD_DIM = 64
MOBA_BLOCK = 256
MOBA_TOPK = 3
ROPE_THETA = 10000.0
GM_GROUPS = 8
GM_DIM = 64
GM_CHUNK = 128
PEER_HEADS = 8
N_KEYS = 128
PEER_TOPK = 16
PAGE_SIZE = 128
EPS = 1e-6
NEG = -1e30

LANES = 128
SUBLANES = 8
V7X_VMEM_BYTES = 64 * 1024 * 1024

TOKEN_TILE = 256
PAGES_PER_STEP = 16
PAGES_PER_BLOCK = MOBA_BLOCK // PAGE_SIZE
PEER_TOKEN_TILE = 256
PEER_EXPERT_ROWS = 8

_NT = (((1,), (1,)), ((), ()))
_TN = (((0,), (0,)), ((), ()))


def _vmem_limit(estimate_bytes):
    return int(min(max(2 * estimate_bytes, 32 * 1024 * 1024), V7X_VMEM_BYTES - 8 * 1024 * 1024))


def _params(semantics, estimate_bytes):
    return pltpu.CompilerParams(dimension_semantics=semantics, vmem_limit_bytes=_vmem_limit(estimate_bytes))


def _mod(x, n):
    return x & (n - 1) if n & (n - 1) == 0 else lax.rem(x, n)


def _rms(x, g):
    return x * lax.rsqrt(jnp.mean(x * x, axis=-1, keepdims=True) + EPS) * g


def _inproj_kernel(n_prompt_tiles, seq, past, dec_seq,
                   x_ref, g1_ref, w_ref, qg_ref, kg_ref, gmg_ref, inv_ref, sgn_ref, smat_ref,
                   q32_ref, k32_ref, kb_ref, v32_ref, vb_ref, gu_ref, gvn_ref, sga_ref, sgb_ref, kbar_ref):
    i = pl.program_id(0)
    tm = x_ref.shape[0]
    aw = q32_ref.shape[1]
    gw = gu_ref.shape[1]
    d = x_ref.shape[1]
    h = _rms(x_ref[...], g1_ref[...])
    z = jnp.dot(h.astype(BF16), w_ref[...], preferred_element_type=F32)
    q = z[:, 0:aw]
    k = z[:, aw:2 * aw]
    v = z[:, 2 * aw:3 * aw]
    u = z[:, 3 * aw:3 * aw + gw]
    gv = z[:, 3 * aw + gw:3 * aw + 2 * gw]
    ga = z[:, 3 * aw + 2 * gw:3 * aw + 2 * gw + d]
    gb = z[:, 3 * aw + 2 * gw + d:3 * aw + 2 * gw + 2 * d]

    r = lax.broadcasted_iota(I32, (tm, LANES), 0)
    is_prompt = jnp.full((tm, LANES), i, I32) < n_prompt_tiles
    pos = jnp.where(is_prompt, _mod(i * tm, seq) + r, past + _mod(r, dec_seq))
    ang = pos.astype(F32) * inv_ref[...]
    cos = jnp.cos(ang)
    sin = jnp.sin(ang) * sgn_ref[...]
    reps = aw // LANES
    cos_w = jnp.concatenate([cos] * reps, axis=1)
    sin_w = jnp.concatenate([sin] * reps, axis=1)
    lane = lax.broadcasted_iota(I32, (tm, aw), 1)
    first_half = (lane & (HEAD_DIM - 1)) < HEAD_DIM // 2
    smat = smat_ref[...]

    def head_norm_rope(t, g):
        sq = t * t
        hi = sq.astype(BF16)
        lo = (sq - hi.astype(F32)).astype(BF16)
        ms = jnp.dot(hi, smat, preferred_element_type=F32) + jnp.dot(lo, smat, preferred_element_type=F32)
        tn = t * lax.rsqrt(ms + EPS) * g
        partner = jnp.where(first_half, pltpu.roll(tn, aw - HEAD_DIM // 2, 1), pltpu.roll(tn, HEAD_DIM // 2, 1))
        return tn * cos_w + partner * sin_w

    qr = head_norm_rope(q, qg_ref[...])
    kr = head_norm_rope(k, kg_ref[...])
    q32_ref[...] = qr
    k32_ref[...] = kr
    kb_ref[...] = kr.astype(BF16)
    v32_ref[...] = v
    vb_ref[...] = v.astype(BF16)
    gu_ref[...] = jax.nn.gelu(u)
    gvn_ref[...] = _rms(jax.nn.gelu(gv), gmg_ref[...])
    sga_ref[...] = jax.nn.sigmoid(ga)
    sgb_ref[...] = jax.nn.sigmoid(gb)
    kbar_ref[0] = jnp.mean(kr, axis=0, keepdims=True)


def _inproj(x, g1, w_in_b, qg, kg, gmg, n_prompt_tiles, seq, past, dec_seq):
    n, d = x.shape
    aw = ATT_HEADS * HEAD_DIM
    gw = GM_GROUPS * GM_DIM
    cols = w_in_b.shape[1]
    tm = TOKEN_TILE
    nt = n // tm
    half = HEAD_DIM // 2
    inv = jnp.exp(-math.log(ROPE_THETA) * jnp.arange(half, dtype=F32) / half)
    inv_l = jnp.tile(inv, LANES // half)[None, :]
    sgn_l = jnp.tile(jnp.concatenate([-jnp.ones((half,), F32), jnp.ones((half,), F32)]), LANES // HEAD_DIM)[None, :]
    head_of = jnp.arange(aw) // HEAD_DIM
    smat = jnp.where(head_of[:, None] == head_of[None, :], 1.0 / HEAD_DIM, 0.0).astype(BF16)
    tok = lambda w: pl.BlockSpec((tm, w), lambda i: (i, 0))
    full = lambda a: pl.BlockSpec(a.shape, lambda i: (0,) * a.ndim)
    qg_w = jnp.tile(qg, ATT_HEADS)[None, :]
    kg_w = jnp.tile(kg, ATT_HEADS)[None, :]
    ins = (x, g1[None, :], w_in_b, qg_w, kg_w, gmg[None, :], inv_l, sgn_l, smat)
    out_shape = (
        jax.ShapeDtypeStruct((n, aw), F32),
        jax.ShapeDtypeStruct((n, aw), F32), jax.ShapeDtypeStruct((n, aw), BF16),
        jax.ShapeDtypeStruct((n, aw), F32), jax.ShapeDtypeStruct((n, aw), BF16),
        jax.ShapeDtypeStruct((n, gw), F32), jax.ShapeDtypeStruct((n, gw), F32),
        jax.ShapeDtypeStruct((n, d), F32), jax.ShapeDtypeStruct((n, d), F32),
        jax.ShapeDtypeStruct((nt, 1, aw), F32),
    )
    out_specs = (tok(aw),) * 5 + (tok(gw),) * 2 + (tok(d),) * 2 + (pl.BlockSpec((1, 1, aw), lambda i: (i, 0, 0)),)
    est = 2 * w_in_b.size * 2 + 2 * tm * d * 4 + 3 * tm * cols * 4 + 2 * tm * cols * 4
    return pl.pallas_call(
        functools.partial(_inproj_kernel, n_prompt_tiles, seq, past, dec_seq),
        grid=(nt,),
        in_specs=[tok(d)] + [full(a) for a in ins[1:]],
        out_specs=out_specs,
        out_shape=out_shape,
        compiler_params=_params(("arbitrary",), est),
        name="inproj",
    )(*ins)


def _topk_mask(s, pos, n_valid, k, axis):
    sel = jnp.zeros(s.shape, F32)
    width = s.shape[axis]
    for r in range(k):
        m = jnp.max(s, axis=axis, keepdims=True)
        idx = jnp.min(jnp.where(s == m, pos, width), axis=axis, keepdims=True)
        hit = pos == idx
        sel = jnp.where(jnp.logical_and(hit, n_valid > r), 1.0, sel)
        s = jnp.where(hit, -jnp.inf, s)
    return sel


def _moba_prompt_kernel(q32_ref, kb_ref, vb_ref, kbar_ref, o_ref, kaug_ref, vt_ref):
    qi = pl.program_id(2)
    tq = q32_ref.shape[0]
    nb = vt_ref.shape[0]
    nbk = kbar_ref.shape[1]
    scale = HEAD_DIM ** -0.5
    lane_k = lax.broadcasted_iota(I32, (tq, LANES), 1)

    @pl.when(qi == 0)
    def _prepare():
        for n in range(nb):
            k = kb_ref[n * tq:(n + 1) * tq, :].astype(F32)
            kaug_ref[0, n] = jnp.where(lane_k < HEAD_DIM, k, jnp.where(lane_k == HEAD_DIM + n, 1.0, 0.0)).astype(BF16)
            kaug_ref[1, n] = jnp.where(lane_k >= HEAD_DIM, k, jnp.where(lane_k == n, 1.0, 0.0)).astype(BF16)
            vt_ref[n] = vb_ref[n * tq:(n + 1) * tq, :].T

    qt = q32_ref[...].T
    kbar = kbar_ref[0]
    lane_b = lax.broadcasted_iota(I32, (nbk, LANES), 1)
    row = lax.broadcasted_iota(I32, (nbk, tq), 0)
    n_past = jnp.full((nbk, tq), qi, I32)
    q_aug = []
    for hh in range(2):
        in_head = (lane_b < HEAD_DIM) if hh == 0 else (lane_b >= HEAD_DIM)
        s = jnp.dot(jnp.where(in_head, kbar, 0.0), qt, precision=lax.Precision.HIGHEST, preferred_element_type=F32)
        s = jnp.where(row < n_past, s, NEG)
        sel = _topk_mask(s, row, n_past, MOBA_TOPK, 0)
        allow = jnp.where(jnp.logical_or(sel > 0.0, row == n_past), 0.0, NEG)
        qh = qt[hh * HEAD_DIM:(hh + 1) * HEAD_DIM] * scale
        q_aug.append(jnp.concatenate([qh, allow] if hh == 0 else [allow, qh], axis=0).astype(BF16))

    key_row = lax.broadcasted_iota(I32, (tq, tq), 0)
    q_col = lax.broadcasted_iota(I32, (tq, tq), 1)
    state = []
    for hh in range(2):
        lg = jnp.dot(kaug_ref[hh, qi], q_aug[hh], preferred_element_type=F32)
        lg = jnp.where(key_row <= q_col, lg, NEG)
        m = jnp.max(lg, axis=0, keepdims=True)
        p = jnp.exp(lg - m)
        l = jnp.sum(p, axis=0, keepdims=True)
        acc = jnp.dot(vt_ref[qi, hh * HEAD_DIM:(hh + 1) * HEAD_DIM, :], p.astype(BF16), preferred_element_type=F32)
        state += [m, l, acc]

    def body(n, st):
        out = []
        for hh in range(2):
            m, l, acc = st[3 * hh:3 * hh + 3]
            lg = jnp.dot(kaug_ref[hh, n], q_aug[hh], preferred_element_type=F32)
            m_new = jnp.maximum(m, jnp.max(lg, axis=0, keepdims=True))
            alpha = jnp.exp(m - m_new)
            p = jnp.exp(lg - m_new)
            l = alpha * l + jnp.sum(p, axis=0, keepdims=True)
            acc = alpha * acc + jnp.dot(vt_ref[n, hh * HEAD_DIM:(hh + 1) * HEAD_DIM, :], p.astype(BF16),
                                        preferred_element_type=F32)
            out += [m_new, l, acc]
        return tuple(out)

    st = lax.fori_loop(0, qi, body, tuple(state))
    out_t = jnp.concatenate([st[2] / st[1], st[5] / st[4]], axis=0)
    o_ref[...] = out_t.T.astype(o_ref.dtype)


def _moba_prompt(q32, kb, vb, kbar, batch, seq):
    aw = ATT_HEADS * HEAD_DIM
    tq = MOBA_BLOCK
    nb = seq // tq
    nbk = kbar.shape[1]
    n_pairs = aw // LANES
    qspec = pl.BlockSpec((tq, LANES), lambda b, hp, qi: (b * nb + qi, hp))
    kvspec = pl.BlockSpec((seq, LANES), lambda b, hp, qi: (b, hp))
    est = 2 * 2 * seq * LANES * 2 + 3 * seq * LANES * 2 + 12 * tq * tq * 4
    return pl.pallas_call(
        _moba_prompt_kernel,
        grid=(batch, n_pairs, nb),
        in_specs=[qspec, kvspec, kvspec, pl.BlockSpec((1, nbk, LANES), lambda b, hp, qi: (b, 0, hp))],
        out_specs=qspec,
        out_shape=jax.ShapeDtypeStruct((batch * seq, aw), BF16),
        scratch_shapes=[pltpu.VMEM((2, nb, tq, LANES), BF16), pltpu.VMEM((nb, LANES, tq), BF16)],
        compiler_params=_params(("arbitrary",) * 3, est),
        name="moba_prompt",
    )(q32, kb, vb, kbar)


def _page_index(j, b, s, pt_ref):
    return (pt_ref[b, s * PAGES_PER_STEP + j], 0, 0, 0)


def _cache_kbar_kernel(pt_ref, *refs):
    pages, o_ref = refs[:PAGES_PER_STEP], refs[PAGES_PER_STEP]
    s = pl.program_id(1)
    blocks_per_step = PAGES_PER_STEP // PAGES_PER_BLOCK
    lane = lax.broadcasted_iota(I32, o_ref.shape, 2)

    @pl.when(s == 0)
    def _zero():
        o_ref[...] = jnp.zeros_like(o_ref)

    acc = o_ref[...]
    for blk in range(blocks_per_step):
        tot = pages[blk * PAGES_PER_BLOCK][...]
        for j in range(1, PAGES_PER_BLOCK):
            tot = tot + pages[blk * PAGES_PER_BLOCK + j][...]
        mean = jnp.sum(tot, axis=-1, keepdims=True) * (1.0 / MOBA_BLOCK)
        acc = jnp.where(lane == s * blocks_per_step + blk, mean, acc)
    o_ref[...] = acc


def _cache_kbar(cache_t, page_table):
    db, n_pages = page_table.shape
    steps = n_pages // PAGES_PER_STEP
    slab = (None, ATT_HEADS, HEAD_DIM, PAGE_SIZE)
    grid_spec = pltpu.PrefetchScalarGridSpec(
        num_scalar_prefetch=1,
        grid=(db, steps),
        in_specs=[pl.BlockSpec(slab, functools.partial(_page_index, j)) for j in range(PAGES_PER_STEP)],
        out_specs=pl.BlockSpec((None, ATT_HEADS, HEAD_DIM, LANES), lambda b, s, pt: (b, 0, 0, 0)),
    )
    est = 2 * PAGES_PER_STEP * ATT_HEADS * HEAD_DIM * PAGE_SIZE * 4
    return pl.pallas_call(
        _cache_kbar_kernel,
        grid_spec=grid_spec,
        out_shape=jax.ShapeDtypeStruct((db, ATT_HEADS, HEAD_DIM, LANES), F32),
        compiler_params=_params(("arbitrary",) * 2, est),
        name="cache_kbar",
    )(page_table, *([cache_t] * PAGES_PER_STEP))


def _sample_select_kernel(n_blocks, q_ref, kbar_ref, ids_ref):
    q = q_ref[...]
    tq = q.shape[0]
    col = lax.broadcasted_iota(I32, (tq, LANES), 1)
    for h in range(ATT_HEADS):
        sc = jnp.dot(q[:, h * HEAD_DIM:(h + 1) * HEAD_DIM], kbar_ref[h], precision=lax.Precision.HIGHEST,
                     preferred_element_type=F32)
        sc = jnp.where(col < n_blocks, sc, -jnp.inf)
        ids = jnp.zeros((tq, LANES), I32)
        for r in range(MOBA_TOPK):
            m = jnp.max(sc, axis=1, keepdims=True)
            idx = jnp.min(jnp.where(sc == m, col, LANES), axis=1, keepdims=True)
            ids = jnp.where(col == r, idx, ids)
            sc = jnp.where(col == idx, -jnp.inf, sc)
        ids_ref[h] = ids


def _sample_select(q32, kbar_s, row0, dec_seq, n_blocks):
    db = kbar_s.shape[0]
    aw = ATT_HEADS * HEAD_DIM
    blk0 = row0 // dec_seq
    return pl.pallas_call(
        functools.partial(_sample_select_kernel, n_blocks),
        grid=(db,),
        in_specs=[pl.BlockSpec((dec_seq, aw), lambda b: (blk0 + b, 0)),
                  pl.BlockSpec((None, ATT_HEADS, HEAD_DIM, LANES), lambda b: (b, 0, 0, 0))],
        out_specs=pl.BlockSpec((None, ATT_HEADS, dec_seq, LANES), lambda b: (b, 0, 0, 0)),
        out_shape=jax.ShapeDtypeStruct((db, ATT_HEADS, dec_seq, LANES), I32),
        compiler_params=_params(("arbitrary",), 4 * ATT_HEADS * HEAD_DIM * LANES * 4),
        name="sample_select",
    )(q32, kbar_s)


def _moba_sample_kernel(dec_seq, n_pages, ids_ref, pt_ref, qt_ref, knt_ref, vnt_ref, ck_ref, cv_ref, o_ref,
                        kbuf, vbuf, ksem, vsem):
    g = pl.program_id(0)
    n_slabs = MOBA_TOPK * PAGES_PER_BLOCK
    scale = HEAD_DIM ** -0.5

    def gather(step, slot, start):
        b = step // ATT_HEADS
        h = _mod(step, ATT_HEADS)
        for q in range(dec_seq):
            for r in range(MOBA_TOPK):
                for pg in range(PAGES_PER_BLOCK):
                    j = r * PAGES_PER_BLOCK + pg
                    if start:
                        blk = ids_ref[(step * dec_seq + q) * MOBA_TOPK + r]
                        page = pt_ref[b * n_pages + blk * PAGES_PER_BLOCK + pg]
                        pltpu.make_async_copy(ck_ref.at[page, h], kbuf.at[slot, q, j], ksem.at[slot]).start()
                        pltpu.make_async_copy(cv_ref.at[page, h], vbuf.at[slot, q, j], vsem.at[slot]).start()
                    else:
                        pltpu.make_async_copy(ck_ref.at[0, 0], kbuf.at[slot, q, j], ksem.at[slot]).wait()
                        pltpu.make_async_copy(cv_ref.at[0, 0], vbuf.at[slot, q, j], vsem.at[slot]).wait()

    slot = _mod(g, 2)

    @pl.when(g == 0)
    def _first():
        gather(g, slot, True)

    @pl.when(g + 1 < pl.num_programs(0))
    def _next():
        gather(g + 1, 1 - slot, True)

    gather(g, slot, False)

    lane = lax.broadcasted_iota(I32, (1, PAGE_SIZE), 1)
    lane_o = lax.broadcasted_iota(I32, (HEAD_DIM, LANES), 1)
    qt = qt_ref[...]
    knt = knt_ref[...]
    vnt = vnt_ref[...]
    out_t = jnp.zeros((HEAD_DIM, LANES), F32)
    for q in range(dec_seq):
        qcol = jnp.broadcast_to(qt[:, q:q + 1], (HEAD_DIM, PAGE_SIZE)) * scale
        rows = [jnp.sum(kbuf[slot, q, j] * qcol, axis=0, keepdims=True) for j in range(n_slabs)]
        rows.append(jnp.where(lane <= q, jnp.sum(knt * qcol, axis=0, keepdims=True), NEG))
        m = rows[0]
        for row in rows[1:]:
            m = jnp.maximum(m, row)
        m = jnp.max(m, axis=1, keepdims=True)
        ps = [jnp.exp(row - m) for row in rows]
        tot = ps[0]
        for p in ps[1:]:
            tot = tot + p
        l = jnp.sum(tot, axis=1, keepdims=True)
        acc = vnt * ps[n_slabs]
        for j in range(n_slabs):
            acc = acc + vbuf[slot, q, j] * ps[j]
        out_t = jnp.where(lane_o == q, jnp.sum(acc, axis=1, keepdims=True) / l, out_t)
    o_ref[...] = out_t


def _moba_sample(ids, page_table, q_t, knew_t, vnew_t, cache_k_t, cache_v_t, dec_seq):
    steps = q_t.shape[0]
    n_pages = page_table.shape[1]
    n_slabs = MOBA_TOPK * PAGES_PER_BLOCK
    tile = pl.BlockSpec((None, HEAD_DIM, LANES), lambda g, ids, pt: (g, 0, 0))
    any_spec = pl.BlockSpec(memory_space=pl.ANY)
    buf = pltpu.VMEM((2, dec_seq, n_slabs, HEAD_DIM, PAGE_SIZE), F32)
    grid_spec = pltpu.PrefetchScalarGridSpec(
        num_scalar_prefetch=2,
        grid=(steps,),
        in_specs=[tile, tile, tile, any_spec, any_spec],
        out_specs=tile,
        scratch_shapes=[buf, buf, pltpu.SemaphoreType.DMA((2,)), pltpu.SemaphoreType.DMA((2,))],
    )
    est = 2 * 2 * dec_seq * n_slabs * HEAD_DIM * PAGE_SIZE * 4
    return pl.pallas_call(
        functools.partial(_moba_sample_kernel, dec_seq, n_pages),
        grid_spec=grid_spec,
        out_shape=jax.ShapeDtypeStruct((steps, HEAD_DIM, LANES), F32),
        compiler_params=_params(("arbitrary",), est),
        name="moba_sample",
    )(ids.reshape(-1), page_table.reshape(-1), q_t, knew_t, vnew_t, cache_k_t, cache_v_t)


def _head_major_t(rows, db, dec_seq):
    t = rows.reshape(db, dec_seq, ATT_HEADS, HEAD_DIM).transpose(0, 2, 3, 1)
    t = jnp.pad(t, ((0, 0), (0, 0), (0, 0), (0, LANES - dec_seq)))
    return t.reshape(db * ATT_HEADS, HEAD_DIM, LANES)


def _merge_kernel(n_prompt_chunks, dec_seq,
                  x_ref, att_ref, gu_ref, gvn_ref, sga_ref, sgb_ref, ws_ref, bias_ref,
                  wa_ref, wb_ref, wo_ref, g2_ref, wq_ref, keys_ref,
                  x1_ref, h2_ref, st_ref):
    i = pl.program_id(0)
    c = x_ref.shape[0]
    gw = gu_ref.shape[1]
    r = lax.broadcasted_iota(I32, (c, c), 0)
    cc = lax.broadcasted_iota(I32, (c, c), 1)
    is_prompt = jnp.full((c, c), i, I32) < n_prompt_chunks
    same_seq = (r // dec_seq) == (cc // dec_seq)
    keep = jnp.logical_and(cc <= r, jnp.logical_or(is_prompt, same_seq))
    lane = lax.broadcasted_iota(I32, (1, LANES), 1)
    lo_lanes = lane < GM_DIM
    gvn_b = gvn_ref[...].astype(BF16)
    mixed = []
    for pr in range(gw // LANES):
        v_pair = gvn_b[:, pr * LANES:(pr + 1) * LANES]
        halves = []
        for hh in range(2):
            w = jnp.where(keep, ws_ref[2 * pr + hh], 0.0).astype(BF16)
            halves.append(jnp.dot(w, v_pair, preferred_element_type=F32))
        mixed.append(jnp.where(lo_lanes, halves[0], halves[1]))
    mixed = jnp.concatenate(mixed, axis=1) + bias_ref[...]
    sg = gu_ref[...] * mixed
    a = jnp.dot(att_ref[...], wa_ref[...], preferred_element_type=F32)
    b = jnp.dot(sg.astype(BF16), wb_ref[...], preferred_element_type=F32)
    merged = sga_ref[...] * a + sgb_ref[...] * b
    x1 = x_ref[...] + jnp.dot(merged.astype(BF16), wo_ref[...], preferred_element_type=F32)
    x1_ref[...] = x1
    h2 = _rms(x1, g2_ref[...]).astype(BF16)
    h2_ref[...] = h2
    qp = jnp.dot(h2, wq_ref[...], preferred_element_type=F32)
    kd = keys_ref.shape[2]
    for hs in range(st_ref.shape[0]):
        st_ref[hs] = lax.dot_general(keys_ref[hs], qp[:, hs * kd:(hs + 1) * kd], _NT,
                                     precision=lax.Precision.HIGHEST, preferred_element_type=F32)


def _merge(x, att, gu, gvn, sga, sgb, ws2, bias2, wa, wb, wo, g2, wq, keys, n_prompt_chunks, dec_seq):
    n, d = x.shape
    c = GM_CHUNK
    aw = att.shape[1]
    gw = gu.shape[1]
    nhs = keys.shape[0]
    tok = lambda w: pl.BlockSpec((c, w), lambda i: (i, 0))
    full = lambda a: pl.BlockSpec(a.shape, lambda i: (0,) * a.ndim)
    variant = lambda i: jnp.where(i >= n_prompt_chunks, 1, 0)
    in_specs = [tok(d), tok(aw), tok(gw), tok(gw), tok(d), tok(d),
                pl.BlockSpec((None, GM_GROUPS, c, c), lambda i: (variant(i), 0, 0, 0)),
                pl.BlockSpec((None, c, gw), lambda i: (variant(i), 0, 0)),
                full(wa), full(wb), full(wo), full(g2), full(wq), full(keys)]
    out_shape = (jax.ShapeDtypeStruct((n, d), F32), jax.ShapeDtypeStruct((n, d), BF16),
                 jax.ShapeDtypeStruct((nhs, N_KEYS, n), F32))
    out_specs = (tok(d), tok(d), pl.BlockSpec((nhs, N_KEYS, c), lambda i: (0, 0, i)))
    est = 2 * 2 * (wa.size + wb.size + wo.size + wq.size) + 4 * keys.size * 2 + 16 * c * d * 4
    return pl.pallas_call(
        functools.partial(_merge_kernel, n_prompt_chunks, dec_seq),
        grid=(n // c,),
        in_specs=in_specs,
        out_specs=out_specs,
        out_shape=out_shape,
        compiler_params=_params(("arbitrary",), est),
        name="merge",
    )(x, att, gu, gvn, sga, sgb, ws2, bias2, wa, wb, wo, g2, wq, keys)


def _extract_top(vals, row, k, with_rank=False):
    out = []
    height = vals.shape[0]
    rank = jnp.full(vals.shape, float(k), F32)
    for r in range(k):
        m = jnp.max(vals, axis=0, keepdims=True)
        idx = jnp.min(jnp.where(vals == m, row, height), axis=0, keepdims=True)
        hit = row == idx
        vals = jnp.where(hit, -jnp.inf, vals)
        if with_rank:
            rank = jnp.where(hit, float(r), rank)
        out.append(m)
    return (out, rank) if with_rank else out


_PAIR_RANKS = [(a, b) for a in range(PEER_TOPK) for b in range(PEER_TOPK) if (a + 1) * (b + 1) <= PEER_TOPK]
_N_CAND_ROWS = -(-len(_PAIR_RANKS) // SUBLANES) * SUBLANES


def _peer_select_kernel(st_ref, rank1_ref, e1_ref, depth_ref, e0_ref, cand_ref):
    tn = st_ref.shape[2]
    row = lax.broadcasted_iota(I32, (N_KEYS, tn), 0)
    crow = lax.broadcasted_iota(I32, (_N_CAND_ROWS, tn), 0)
    cand_ref[...] = jnp.full((_N_CAND_ROWS, tn), -jnp.inf, F32)
    for h in range(PEER_HEADS):
        s0 = st_ref[2 * h]
        s1 = st_ref[2 * h + 1]
        top0 = _extract_top(s0, row, PEER_TOPK)
        top1, rank1 = _extract_top(s1, row, PEER_TOPK, with_rank=True)
        for c, (a, b) in enumerate(_PAIR_RANKS):
            cand_ref[c:c + 1, :] = top0[a] + top1[b]
        best = _extract_top(cand_ref[...], crow, PEER_TOPK)
        tau = best[-1]
        z = jnp.zeros((1, tn), F32)
        for t in best:
            z = z + jnp.exp(t - best[0])
        depth = jnp.full((N_KEYS, tn), -1.0, F32)
        for b in range(PEER_TOPK):
            depth = depth + jnp.where(s0 + top1[b] >= tau, 1.0, 0.0)
        rank1_ref[h] = rank1.astype(BF16)
        e1_ref[h] = jnp.exp(s1 - top1[0]).astype(BF16)
        depth_ref[h] = depth
        e0_ref[h] = jnp.exp(s0 - top0[0]) / z


def _peer_select(st):
    nhs, nk, n = st.shape
    tn = LANES
    ospec = pl.BlockSpec((PEER_HEADS, nk, tn), lambda i: (0, 0, i))
    half = jax.ShapeDtypeStruct((PEER_HEADS, nk, n), BF16)
    full = jax.ShapeDtypeStruct((PEER_HEADS, nk, n), F32)
    return pl.pallas_call(
        _peer_select_kernel,
        grid=(n // tn,),
        in_specs=[pl.BlockSpec((nhs, nk, tn), lambda i: (0, 0, i))],
        out_specs=(ospec,) * 4,
        out_shape=(half, half, full, full),
        scratch_shapes=[pltpu.VMEM((_N_CAND_ROWS, tn), F32)],
        compiler_params=_params(("arbitrary",), 8 * nhs * nk * tn * 4),
        name="peer_select",
    )(st)


def _peer_dense_kernel(h2_ref, rank1_ref, e1_ref, depth_ref, e0_ref, u_ref, v_ref, o_ref, acc_ref):
    e = pl.program_id(1)
    tn = h2_ref.shape[0]

    @pl.when(e == 0)
    def _zero():
        acc_ref[...] = jnp.zeros_like(acc_ref)

    act = jax.nn.gelu(lax.dot_general(u_ref[...], h2_ref[...], _NT, preferred_element_type=F32)).astype(BF16)
    parts = []
    for il in range(PEER_EXPERT_ROWS):
        i = e * PEER_EXPERT_ROWS + il
        w = jnp.zeros((N_KEYS, tn), BF16)
        for h in range(PEER_HEADS):
            depth_row = depth_ref[h, pl.ds(i, 1), :].astype(BF16)
            e0_row = e0_ref[h, pl.ds(i, 1), :].astype(BF16)
            w = w + jnp.where(rank1_ref[h] <= depth_row, e1_ref[h] * e0_row, jnp.zeros((N_KEYS, tn), BF16))
        parts.append(w * act[il * N_KEYS:(il + 1) * N_KEYS, :])
    p = jnp.concatenate(parts, axis=0)
    acc_ref[...] += lax.dot_general(p, v_ref[...], _TN, preferred_element_type=F32)

    @pl.when(e == pl.num_programs(1) - 1)
    def _out():
        o_ref[...] = acc_ref[...]


def _peer_dense(h2, rank1, e1, depth, e0, eu_b, ev_b):
    n, d = h2.shape
    nh, nk, _ = rank1.shape
    n_exp = eu_b.shape[0]
    tn = PEER_TOKEN_TILE
    te = PEER_EXPERT_ROWS * N_KEYS
    sspec = pl.BlockSpec((nh, nk, tn), lambda t, e: (0, 0, t))
    wspec = pl.BlockSpec((te, d), lambda t, e: (e, 0))
    est = 2 * nh * nk * tn * 12 + 2 * 2 * te * d * 2 + 3 * tn * d * 4 + 4 * te * tn * 4
    return pl.pallas_call(
        _peer_dense_kernel,
        grid=(n // tn, n_exp // te),
        in_specs=[pl.BlockSpec((tn, d), lambda t, e: (t, 0)), sspec, sspec, sspec, sspec, wspec, wspec],
        out_specs=pl.BlockSpec((tn, d), lambda t, e: (t, 0)),
        out_shape=jax.ShapeDtypeStruct((n, d), F32),
        scratch_shapes=[pltpu.VMEM((tn, d), F32)],
        compiler_params=_params(("arbitrary",) * 2, est),
        name="peer_dense",
    )(h2, rank1, e1, depth, e0, eu_b, ev_b)


def _ple_kernel(x1_ref, peer_ref, p_ref, g_ref, wg_ref, wp_ref, o_ref):
    x2 = x1_ref[...] + peer_ref[...]
    gate = jax.nn.sigmoid(jnp.dot(_rms(x2, g_ref[...]).astype(BF16), wg_ref[...], preferred_element_type=F32))
    emb = jnp.dot(p_ref[...].astype(BF16), wp_ref[...], preferred_element_type=F32)
    o_ref[...] = x2 + gate * emb


def _ple(x1, peer, p, g, wg, wp):
    n, d = x1.shape
    pd = p.shape[1]
    tm = TOKEN_TILE
    tok = lambda w: pl.BlockSpec((tm, w), lambda i: (i, 0))
    full = lambda a: pl.BlockSpec(a.shape, lambda i: (0,) * a.ndim)
    est = 2 * 2 * (wg.size + wp.size) + 8 * tm * d * 4
    return pl.pallas_call(
        _ple_kernel,
        grid=(n // tm,),
        in_specs=[tok(d), tok(d), tok(pd), full(g), full(wg), full(wp)],
        out_specs=tok(d),
        out_shape=jax.ShapeDtypeStruct((n, d), F32),
        compiler_params=_params(("arbitrary",), est),
        name="ple",
    )(x1, peer, p, g, wg, wp)


def _layer(x, p, cache_k_l, cache_v_l, page_table, dims, w):
    (norm1_g, w_in, q_norm_g, k_norm_g, gm_norm_g, gm_ws, gm_bias, w_br_a, w_br_b, w_out, norm2_g,
     peer_wq, peer_keys, peer_u, peer_v, ple_norm_g, ple_gate_w, ple_w) = w
    batch, seq, db, dec_seq, past = dims
    n_prompt = batch * seq
    gw = GM_GROUPS * GM_DIM
    aw = ATT_HEADS * HEAD_DIM

    (q32, k32, kb, v32, vb, gu, gvn, sga, sgb, kbar) = _inproj(
        x, norm1_g, w_in.astype(BF16), q_norm_g, k_norm_g, gm_norm_g,
        n_prompt // TOKEN_TILE, seq, past, dec_seq)

    nb = seq // MOBA_BLOCK
    kbar_p = jnp.pad(kbar[:batch * nb].reshape(batch, nb, aw), ((0, 0), (0, HEAD_DIM - nb), (0, 0)))
    att_p = _moba_prompt(q32, kb, vb, kbar_p, batch, seq)

    cache_k_t = cache_k_l.transpose(0, 2, 3, 1)
    cache_v_t = cache_v_l.transpose(0, 2, 3, 1)
    kbar_s = _cache_kbar(cache_k_t, page_table)
    ids = _sample_select(q32, kbar_s, n_prompt, dec_seq, past // MOBA_BLOCK)[..., :MOBA_TOPK]
    out_t = _moba_sample(ids, page_table, _head_major_t(q32[n_prompt:], db, dec_seq),
                         _head_major_t(k32[n_prompt:], db, dec_seq), _head_major_t(v32[n_prompt:], db, dec_seq),
                         cache_k_t, cache_v_t, dec_seq)
    att_s = out_t[:, :, :dec_seq].reshape(db, ATT_HEADS, HEAD_DIM, dec_seq).transpose(0, 3, 1, 2)
    att = jnp.concatenate([att_p, att_s.reshape(db * dec_seq, aw).astype(BF16)], axis=0)

    reps = GM_CHUNK // dec_seq
    ws2 = jnp.stack([gm_ws, jnp.tile(gm_ws[:, :dec_seq, :dec_seq], (1, reps, reps))])
    bias_rows = jnp.stack([gm_bias, jnp.tile(gm_bias[:, :dec_seq], (1, reps))])
    bias2 = jnp.broadcast_to(bias_rows.transpose(0, 2, 1)[:, :, :, None], (2, GM_CHUNK, GM_GROUPS, GM_DIM))
    bias2 = bias2.reshape(2, GM_CHUNK, gw)
    keys = peer_keys.reshape(PEER_HEADS * 2, N_KEYS, -1)

    x1, h2, st = _merge(x, att, gu, gvn, sga, sgb, ws2, bias2, w_br_a.astype(BF16), w_br_b.astype(BF16),
                        w_out.astype(BF16), norm2_g[None, :], peer_wq.astype(BF16), keys,
                        n_prompt // GM_CHUNK, dec_seq)
    rank1, e1, depth, e0 = _peer_select(st)
    peer = _peer_dense(h2, rank1, e1, depth, e0, peer_u.astype(BF16), peer_v.astype(BF16))
    y = _ple(x1, peer, p, ple_norm_g[None, :], ple_gate_w.astype(BF16), ple_w.astype(BF16))
    return y, k32, v32, gvn


def kernel(x_prompt, x_sample, cache_k, cache_v, page_table, p_prompt, p_sample, norm1_g, w_in, q_norm_g, k_norm_g, gm_norm_g, gm_ws, gm_bias, w_br_a, w_br_b, w_out, norm2_g, peer_wq, peer_keys, peer_u, peer_v, ple_norm_g, ple_gate_w, ple_w):
    batch, seq, d = x_prompt.shape
    db, dec_seq, _ = x_sample.shape
    depth = w_in.shape[0]
    n_pages = page_table.shape[1]
    past = n_pages * PAGE_SIZE
    n_prompt, n_sample = batch * seq, db * dec_seq
    gw = GM_GROUPS * GM_DIM
    assert seq % MOBA_BLOCK == 0 and TOKEN_TILE == MOBA_BLOCK and n_sample % TOKEN_TILE == 0
    assert seq // MOBA_BLOCK <= HEAD_DIM and past // MOBA_BLOCK <= LANES
    assert past % MOBA_BLOCK == 0 and dec_seq <= MOBA_BLOCK and past // MOBA_BLOCK >= MOBA_TOPK
    assert GM_CHUNK % dec_seq == 0 and dec_seq % SUBLANES == 0 and n_pages % PAGES_PER_STEP == 0
    assert peer_keys.shape[3] == N_KEYS and peer_u.shape[1] == N_KEYS * N_KEYS
    dims = (batch, seq, db, dec_seq, past)

    x = jnp.concatenate([x_prompt.reshape(n_prompt, d), x_sample.reshape(n_sample, d)], axis=0)
    outs = ([], [], [], [], [], [])
    for l in range(depth):
        w = (norm1_g[l], w_in[l], q_norm_g[l], k_norm_g[l], gm_norm_g[l], gm_ws[l], gm_bias[l], w_br_a[l],
             w_br_b[l], w_out[l], norm2_g[l], peer_wq[l], peer_keys[l], peer_u[l], peer_v[l], ple_norm_g[l],
             ple_gate_w[l], ple_w[l])
        p = jnp.concatenate([p_prompt[l].reshape(n_prompt, -1), p_sample[l].reshape(n_sample, -1)], axis=0)
        x, k32, v32, gvn = _layer(x, p, cache_k[l], cache_v[l], page_table, dims, w)
        start = ((seq - 1) // GM_CHUNK) * GM_CHUNK
        outs[0].append(k32[:n_prompt].reshape(batch, seq, ATT_HEADS, HEAD_DIM))
        outs[1].append(v32[:n_prompt].reshape(batch, seq, ATT_HEADS, HEAD_DIM))
        outs[2].append(k32[n_prompt:].reshape(db, dec_seq, ATT_HEADS, HEAD_DIM))
        outs[3].append(v32[n_prompt:].reshape(db, dec_seq, ATT_HEADS, HEAD_DIM))
        outs[4].append(gvn[:n_prompt].reshape(batch, seq, gw)[:, start:])
        outs[5].append(gvn[n_prompt:].reshape(db, dec_seq, gw))
    y_p = x[:n_prompt].reshape(batch, seq, d)
    y_s = x[n_prompt:].reshape(db, dec_seq, d)
    return (y_p, y_s) + tuple(jnp.stack(o) for o in outs)
```
